```python
import jax, jax.numpy as jnp
from jax import lax
import numpy as np

D_MODEL = 1024
BATCH = 8
SEQ = 4096
DEPTH = 1
DEC_BATCH = 8
DEC_SEQ = 16
PAST_LEN = 4096

CHUNK = 64
D_MIX = D_MODEL
RET_HEADS = 4
RET_DK = 128
RET_DV = 128
RET_WIDTH = RET_HEADS * RET_DV
CONV_WIDTH = D_MIX - RET_WIDTH
CONV_GROUPS = 8
CONV_K = 3
D_FF = 4 * D_MODEL
FFN_CONV_K = 3
ROPE_BASE = 10000.0
EPS = 1e-6
IN_COLS = 3 * RET_HEADS * RET_DK + RET_WIDTH + 3 * CONV_WIDTH

kernel_name = "hybrid_retention_shortconv_streaming_step"


def rms_norm(x, w):
    xf = x.astype(jnp.float32)
    var = jnp.mean(xf * xf, axis=-1, keepdims=True)
    return (xf * lax.rsqrt(var + EPS)).astype(x.dtype) * w


def rotary(x, pos):
    d = x.shape[-1]
    inv = ROPE_BASE ** (-jnp.arange(0, d, 2, dtype=jnp.float32) / d)
    ang = pos.astype(jnp.float32)[:, None] * inv[None, :]
    cos = jnp.cos(ang)[:, None, :]
    sin = jnp.sin(ang)[:, None, :]
    xf = x.astype(jnp.float32)
    x1, x2 = xf[..., : d // 2], xf[..., d // 2:]
    return jnp.concatenate([x1 * cos - x2 * sin, x2 * cos + x1 * sin], axis=-1).astype(x.dtype)


def causal_dwconv(x, buf, w):
    k = w.shape[0]
    L = x.shape[1]
    xp = jnp.concatenate([buf.astype(x.dtype), x], axis=1)
    y = sum(xp[:, i:i + L] * w[i] for i in range(k))
    return y, xp[:, L:]


def retention_chunkwise(q, k, v, s0):
    C = q.shape[2]
    dt = q.dtype
    lg = jnp.log(1.0 - 2.0 ** (-5.0 - jnp.arange(RET_HEADS, dtype=jnp.float32)))
    idx = jnp.arange(C, dtype=jnp.float32)
    dmat = jnp.exp(lg[:, None, None] * jnp.abs(idx[:, None] - idx[None, :])).astype(dt)
    to_end = jnp.exp(lg[None, :] * (C - 1 - idx)[:, None]).astype(dt)
    from_start = jnp.exp(lg[None, :] * (idx + 1)[:, None]).astype(dt)
    g_chunk = jnp.exp(lg * C).astype(dt)[:, None, None]

    scores = jnp.einsum('bnihd,bnjhd->bnhij', q, k) * dmat
    intra = jnp.einsum('bnhij,bnjhe->bnihe', scores, v)
    upd = jnp.einsum('bnjhd,bnjhe->bnhde', k * to_end[None, None, :, :, None], v)

    def step(s, u):
        return g_chunk * s + u, s

    s_final, s_prev = lax.scan(step, s0.astype(upd.dtype), jnp.moveaxis(upd, 1, 0))
    s_prev = jnp.moveaxis(s_prev, 0, 1)
    cross = jnp.einsum('bnihd,bnhde->bnihe', q, s_prev) * from_start[None, None, :, :, None]
    return intra + cross, s_final


def layer(x, pos, chunk, s_ret, c_conv, c_ffn,
          w_in, w_out, conv_w, ret_norm_w, pre_mix_w, post_mix_w,
          pre_ffn_w, post_ffn_w, w_up, w_gate, ffn_conv_w, w_down):
    b, L, _ = x.shape
    n_chunks = L // chunk
    h = rms_norm(x, pre_mix_w)
    p = h @ w_in
    qd = RET_HEADS * RET_DK
    q, k, v, g, bg, cg, hc = jnp.split(
        p, np.cumsum([qd, qd, RET_WIDTH, RET_WIDTH, CONV_WIDTH, CONV_WIDTH]).tolist(), axis=-1)

    q = rotary(q.reshape(b, L, RET_HEADS, RET_DK), pos)
    k = rotary(k.reshape(b, L, RET_HEADS, RET_DK), pos) * (RET_DK ** -0.5)
    v = v.reshape(b, L, RET_HEADS, RET_DV)
    o, s_new = retention_chunkwise(
        q.reshape(b, n_chunks, chunk, RET_HEADS, RET_DK),
        k.reshape(b, n_chunks, chunk, RET_HEADS, RET_DK),
        v.reshape(b, n_chunks, chunk, RET_HEADS, RET_DV), s_ret)
    o = rms_norm(o.reshape(b, L, RET_HEADS, RET_DV), ret_norm_w.reshape(RET_HEADS, RET_DV))
    o = o.reshape(b, L, RET_WIDTH) * jax.nn.silu(g)

    yc, c_conv_new = causal_dwconv(cg * hc, c_conv, conv_w)
    yc = bg * yc

    x = x + rms_norm(jnp.concatenate([o, yc], axis=-1) @ w_out, post_mix_w)

    h2 = rms_norm(x, pre_ffn_w)
    u, c_ffn_new = causal_dwconv(h2 @ w_up, c_ffn, ffn_conv_w)
    f = (jax.nn.gelu(u, approximate=True) * (h2 @ w_gate)) @ w_down
    x = x + rms_norm(f, post_ffn_w)
    return x, s_new.astype(x.dtype), c_conv_new, c_ffn_new


def setup_inputs(seed: int = 0) -> dict:
    key = jax.random.key(seed)
    ks = jax.random.split(key, 20)
    nrm = lambda k, s, sc: jax.random.normal(k, s, jnp.float32) * sc
    gain = lambda k, n: 1.0 + nrm(k, (DEPTH, n), 0.05)
    return {
        "x_prompt": nrm(ks[0], (BATCH, SEQ, D_MODEL), 1.0),
        "x_sample": nrm(ks[1], (DEC_BATCH, DEC_SEQ, D_MODEL), 1.0),
        "state_ret": nrm(ks[2], (DEPTH, DEC_BATCH, RET_HEADS, RET_DK, RET_DV), 0.05),
        "cache_conv": nrm(ks[3], (DEPTH, DEC_BATCH, CONV_K - 1, CONV_WIDTH), 1.0),
        "cache_ffn_conv": nrm(ks[4], (DEPTH, DEC_BATCH, FFN_CONV_K - 1, D_FF), 1.0),
        "w_in": nrm(ks[5], (DEPTH, D_MODEL, IN_COLS), D_MODEL ** -0.5),
        "w_out": nrm(ks[6], (DEPTH, D_MIX, D_MODEL), D_MIX ** -0.5),
        "conv_w": nrm(ks[7], (DEPTH, CONV_K, CONV_WIDTH), CONV_K ** -0.5),
        "ret_norm_w": gain(ks[8], RET_WIDTH),
        "pre_mix_w": gain(ks[9], D_MODEL),
        "post_mix_w": gain(ks[10], D_MODEL),
        "pre_ffn_w": gain(ks[11], D_MODEL),
        "post_ffn_w": gain(ks[12], D_MODEL),
        "w_up": nrm(ks[13], (DEPTH, D_MODEL, D_FF), D_MODEL ** -0.5),
        "w_gate": nrm(ks[14], (DEPTH, D_MODEL, D_FF), D_MODEL ** -0.5),
        "ffn_conv_w": nrm(ks[15], (DEPTH, FFN_CONV_K, D_FF), FFN_CONV_K ** -0.5),
        "w_down": nrm(ks[16], (DEPTH, D_FF, D_MODEL), D_FF ** -0.5),
    }


def reference(x_prompt, x_sample, state_ret, cache_conv, cache_ffn_conv,
              w_in, w_out, conv_w, ret_norm_w, pre_mix_w, post_mix_w,
              pre_ffn_w, post_ffn_w, w_up, w_gate, ffn_conv_w, w_down):
    bp, lp, _ = x_prompt.shape
    bs, ls, _ = x_sample.shape
    dt = x_prompt.dtype
    pos_p = jnp.arange(lp)
    pos_s = PAST_LEN + jnp.arange(ls)
    zs_ret = jnp.zeros((bp, RET_HEADS, RET_DK, RET_DV), dt)
    zs_conv = jnp.zeros((bp, CONV_K - 1, CONV_WIDTH), dt)
    zs_ffn = jnp.zeros((bp, FFN_CONV_K - 1, D_FF), dt)

    yp, ys = x_prompt, x_sample
    sr_p, cc_p, cf_p, sr_s, cc_s, cf_s = [], [], [], [], [], []
    for l in range(DEPTH):
        w = (w_in[l], w_out[l], conv_w[l], ret_norm_w[l], pre_mix_w[l], post_mix_w[l],
             pre_ffn_w[l], post_ffn_w[l], w_up[l], w_gate[l], ffn_conv_w[l], w_down[l])
        yp, a, b_, c = layer(yp, pos_p, CHUNK, zs_ret, zs_conv, zs_ffn, *w)
        sr_p.append(a); cc_p.append(b_); cf_p.append(c)
        ys, a, b_, c = layer(ys, pos_s, ls, state_ret[l], cache_conv[l], cache_ffn_conv[l], *w)
        sr_s.append(a); cc_s.append(b_); cf_s.append(c)

    return (yp, ys,
            jnp.stack(sr_p), jnp.stack(cc_p), jnp.stack(cf_p),
            jnp.stack(sr_s), jnp.stack(cc_s), jnp.stack(cf_s))
```

```python
import functools

import jax
import jax.numpy as jnp
import numpy as np
from jax import lax
from jax.experimental import pallas as pl
from jax.experimental.pallas import tpu as pltpu

D_MODEL = 1024
PAST_LEN = 4096
CHUNK = 64
RET_HEADS = 4
RET_DK = 128
RET_DV = 128
RET_WIDTH = RET_HEADS * RET_DV
CONV_WIDTH = D_MODEL - RET_WIDTH
CONV_K = 3
D_FF = 4 * D_MODEL
ROPE_BASE = 10000.0
EPS = 1e-6

F32 = jnp.float32
BF16 = jnp.bfloat16

V7X_VMEM_BYTES = 64 * 1024 * 1024
VMEM_LIMIT_BYTES = V7X_VMEM_BYTES - 8 * 1024 * 1024


def _rms(x, w):
    var = jnp.mean(x * x, axis=-1, keepdims=True)
    return x * lax.rsqrt(var + EPS) * w


def _causal_conv3(z, prev, w):
    rows = lax.broadcasted_iota(jnp.int32, z.shape, 0)
    z1 = jnp.where(rows == 0, prev[1:2], pltpu.roll(z, 1, 0))
    z2 = jnp.where(rows == 0, prev[0:1], jnp.where(rows == 1, prev[1:2], pltpu.roll(z, 2, 0)))
    return w[0:1] * z2 + w[1:2] * z1 + w[2:3] * z


def _mixer_kernel(x_ref, s0_ref, cc0_ref, cosq_ref, sinq_ref, cosk_ref, sink_ref,
                  dmat_ref, fs_ref, te_ref, g_ref,
                  w_in_ref, w_out_ref, convw_ref, retw_ref, prew_ref, postw_ref,
                  x1_ref, s_ref, cc_ref, mix_ref, *, bt, t, r):
    @pl.when(pl.program_id(1) == 0)
    def _():
        s_ref[...] = s0_ref[...]
        cc_ref[...] = cc0_ref[...]

    x = x_ref[...].reshape(bt * t, D_MODEL)
    h = _rms(x, prew_ref[...]).astype(BF16)

    def proj(group):
        c0 = group * RET_WIDTH
        return jnp.dot(h, w_in_ref[:, c0:c0 + RET_WIDTH], preferred_element_type=F32)

    q, k, v, g, bg, cg, hc = (proj(i) for i in range(7))

    cosq, sinq = cosq_ref[...], sinq_ref[...]
    cosk, sink = cosk_ref[...], sink_ref[...]
    for bi in range(bt):
        r0 = bi * t
        for hd in range(RET_HEADS):
            c0 = hd * RET_DK
            qh = q[r0:r0 + t, c0:c0 + RET_DK]
            kh = k[r0:r0 + t, c0:c0 + RET_DK]
            qr = (qh * cosq + pltpu.roll(qh, RET_DK // 2, 1) * sinq).astype(BF16)
            kr = kh * cosk + pltpu.roll(kh, RET_DK // 2, 1) * sink
            kb = kr.astype(BF16)
            vb = v[r0:r0 + t, c0:c0 + RET_DV].astype(BF16)
            for ri in range(t // r):
                sl = slice(ri * r, (ri + 1) * r)
                state = s_ref[bi, hd]
                scores = lax.dot_general(qr[sl], kb[sl], (((1,), (1,)), ((), ())),
                                         preferred_element_type=F32) * dmat_ref[hd]
                intra = jnp.dot(scores.astype(BF16), vb[sl], preferred_element_type=F32)
                cross = jnp.dot(qr[sl], state.astype(BF16), preferred_element_type=F32) * fs_ref[hd]
                o = intra + cross
                ke = (kr[sl] * te_ref[hd]).astype(BF16)
                upd = lax.dot_general(ke, vb[sl], (((0,), (0,)), ((), ())),
                                      preferred_element_type=F32)
                s_ref[bi, hd] = g_ref[hd] * state + upd
                on = _rms(o, retw_ref[:, c0:c0 + RET_DV])
                gh = g[r0 + ri * r:r0 + (ri + 1) * r, c0:c0 + RET_DV]
                mix_ref[r0 + ri * r:r0 + (ri + 1) * r, c0:c0 + RET_DV] = (on * jax.nn.silu(gh)).astype(BF16)

        z = cg[r0:r0 + t] * hc[r0:r0 + t]
        yc = _causal_conv3(z, cc_ref[bi], convw_ref[...])
        cc_ref[bi] = z[t - 2:t]
        mix_ref[r0:r0 + t, RET_WIDTH:] = (bg[r0:r0 + t] * yc).astype(BF16)

    m = jnp.dot(mix_ref[...], w_out_ref[...], preferred_element_type=F32)
    x1_ref[...] = (x + _rms(m, postw_ref[...])).reshape(bt, t, D_MODEL)


def _ffn_kernel(x_ref, cf0_ref, w_up_ref, w_gate_ref, w_down_ref, fcw_ref, prew_ref, postw_ref,
                y_ref, cf_ref, act_ref, *, bt, t, fc):
    @pl.when(pl.program_id(1) == 0)
    def _():
        cf_ref[...] = cf0_ref[...]

    x = x_ref[...].reshape(bt * t, D_MODEL)
    h = _rms(x, prew_ref[...]).astype(BF16)
    for j in range(D_FF // fc):
        c0 = j * fc
        u = jnp.dot(h, w_up_ref[:, c0:c0 + fc], preferred_element_type=F32)
        gate = jnp.dot(h, w_gate_ref[:, c0:c0 + fc], preferred_element_type=F32)
        w = fcw_ref[:, c0:c0 + fc]
        for bi in range(bt):
            r0 = bi * t
            ub = u[r0:r0 + t]
            uc = _causal_conv3(ub, cf_ref[bi, :, c0:c0 + fc], w)
            cf_ref[bi, :, c0:c0 + fc] = ub[t - 2:t]
            act = jax.nn.gelu(uc, approximate=True) * gate[r0:r0 + t]
            act_ref[r0:r0 + t, c0:c0 + fc] = act.astype(BF16)
    f = jnp.dot(act_ref[...], w_down_ref[...], preferred_element_type=F32)
    y_ref[...] = (x + _rms(f, postw_ref[...])).reshape(bt, t, D_MODEL)


def _resident(shape):
    zeros = (0,) * len(shape)
    return pl.BlockSpec(shape, lambda b, s: zeros, pipeline_mode=pl.Buffered(1))


def _decay_tables(r, chunk):
    lg = jnp.log(1.0 - 2.0 ** (-5.0 - jnp.arange(RET_HEADS, dtype=F32)))
    idx = jnp.arange(r, dtype=F32)
    cid = jnp.arange(r) // chunk
    visible = (cid[None, :] <= cid[:, None]).astype(F32)
    dmat = jnp.exp(lg[:, None, None] * jnp.abs(idx[:, None] - idx[None, :])) * visible[None]
    from_start = jnp.exp(lg[:, None] * (idx + 1.0)[None, :])
    to_end = jnp.exp(lg[:, None] * (r - 1.0 - idx)[None, :])
    g_win = jnp.exp(lg * r)
    widen = lambda a: jnp.broadcast_to(a[:, :, None], (RET_HEADS, r, RET_DK))
    return dmat, widen(from_start), widen(to_end), g_win


def _rope_tables(pos0, length):
    inv = ROPE_BASE ** (-jnp.arange(0, RET_DK, 2, dtype=F32) / RET_DK)
    ang = (pos0 + jnp.arange(length)).astype(F32)[:, None] * inv[None, :]
    cos, sin = jnp.cos(ang), jnp.sin(ang)
    cos2 = jnp.concatenate([cos, cos], axis=-1)
    sin2 = jnp.concatenate([-sin, sin], axis=-1)
    kscale = RET_DK ** -0.5
    return cos2, sin2, cos2 * kscale, sin2 * kscale


def _mixer(x, s0, cc0, pos0, chunk, w, *, bt, t, r):
    b, length, _ = x.shape
    assert b % bt == 0 and length % t == 0 and t % r == 0 and r % chunk == 0
    cosq, sinq, cosk, sink = _rope_tables(pos0, length)
    dmat, fs, te, g_win = _decay_tables(r, chunk)
    row = lambda n: pl.BlockSpec((t, n), lambda bi, ti: (ti, 0))
    per_stream = lambda *tail: pl.BlockSpec((bt,) + tail, lambda bi, ti: (bi,) + (0,) * len(tail))
    tile = pl.BlockSpec((bt, t, D_MODEL), lambda bi, ti: (bi, ti, 0))
    in_specs = [
        tile,
        per_stream(RET_HEADS, RET_DK, RET_DV),
        per_stream(CONV_K - 1, CONV_WIDTH),
        row(RET_DK), row(RET_DK), row(RET_DK), row(RET_DK),
        _resident(dmat.shape), _resident(fs.shape), _resident(te.shape),
        pl.BlockSpec(memory_space=pltpu.SMEM),
        _resident(w["w_in"].shape), _resident(w["w_out"].shape), _resident(w["conv_w"].shape),
        _resident(w["ret_norm_w"].shape), _resident(w["pre_mix_w"].shape), _resident(w["post_mix_w"].shape),
    ]
    out_specs = [tile, per_stream(RET_HEADS, RET_DK, RET_DV), per_stream(CONV_K - 1, CONV_WIDTH)]
    out_shape = [jax.ShapeDtypeStruct(x.shape, F32), jax.ShapeDtypeStruct(s0.shape, F32),
                 jax.ShapeDtypeStruct(cc0.shape, F32)]
    return pl.pallas_call(
        functools.partial(_mixer_kernel, bt=bt, t=t, r=r),
        grid=(b // bt, length // t),
        in_specs=in_specs, out_specs=out_specs, out_shape=out_shape,
        scratch_shapes=[pltpu.VMEM((bt * t, D_MODEL), BF16)],
        compiler_params=pltpu.CompilerParams(
            dimension_semantics=("arbitrary", "arbitrary"), vmem_limit_bytes=VMEM_LIMIT_BYTES),
        name="mixer",
    )(x, s0, cc0, cosq, sinq, cosk, sink, dmat, fs, te, g_win,
      w["w_in"], w["w_out"], w["conv_w"], w["ret_norm_w"], w["pre_mix_w"], w["post_mix_w"])


def _ffn(x, cf0, w, *, bt, t, fc):
    b, length, _ = x.shape
    assert b % bt == 0 and length % t == 0 and D_FF % fc == 0
    per_stream = pl.BlockSpec((bt, CONV_K - 1, D_FF), lambda bi, ti: (bi, 0, 0))
    tile = pl.BlockSpec((bt, t, D_MODEL), lambda bi, ti: (bi, ti, 0))
    in_specs = [
        tile, per_stream,
        _resident(w["w_up"].shape), _resident(w["w_gate"].shape), _resident(w["w_down"].shape),
        _resident(w["ffn_conv_w"].shape), _resident(w["pre_ffn_w"].shape), _resident(w["post_ffn_w"].shape),
    ]
    return pl.pallas_call(
        functools.partial(_ffn_kernel, bt=bt, t=t, fc=fc),
        grid=(b // bt, length // t),
        in_specs=in_specs, out_specs=[tile, per_stream],
        out_shape=[jax.ShapeDtypeStruct(x.shape, F32), jax.ShapeDtypeStruct(cf0.shape, F32)],
        scratch_shapes=[pltpu.VMEM((bt * t, D_FF), BF16)],
        compiler_params=pltpu.CompilerParams(
            dimension_semantics=("arbitrary", "arbitrary"), vmem_limit_bytes=VMEM_LIMIT_BYTES),
        name="ffn",
    )(x, cf0, w["w_up"], w["w_gate"], w["w_down"], w["ffn_conv_w"], w["pre_ffn_w"], w["post_ffn_w"])


def _layer(x, s0, cc0, cf0, pos0, chunk, w, *, bt, t, r, fc):
    x1, s_new, cc_new = _mixer(x, s0, cc0, pos0, chunk, w, bt=bt, t=t, r=r)
    y, cf_new = _ffn(x1, cf0, w, bt=bt, t=t, fc=fc)
    return y, s_new, cc_new, cf_new


def kernel(x_prompt, x_sample, state_ret, cache_conv, cache_ffn_conv, w_in, w_out, conv_w, ret_norm_w,
           pre_mix_w, post_mix_w, pre_ffn_w, post_ffn_w, w_up, w_gate, ffn_conv_w, w_down):
    assert w_in.shape[0] == 1, "single-layer kernel"
    bp, _, _ = x_prompt.shape
    bs, ls, _ = x_sample.shape
    w = {
        "w_in": w_in[0].astype(BF16), "w_out": w_out[0].astype(BF16),
        "w_up": w_up[0].astype(BF16), "w_gate": w_gate[0].astype(BF16), "w_down": w_down[0].astype(BF16),
        "conv_w": conv_w[0], "ffn_conv_w": ffn_conv_w[0],
        "ret_norm_w": ret_norm_w, "pre_mix_w": pre_mix_w, "post_mix_w": post_mix_w,
        "pre_ffn_w": pre_ffn_w, "post_ffn_w": post_ffn_w,
    }
    zs_ret = jnp.zeros((bp, RET_HEADS, RET_DK, RET_DV), F32)
    zs_conv = jnp.zeros((bp, CONV_K - 1, CONV_WIDTH), F32)
    zs_ffn = jnp.zeros((bp, CONV_K - 1, D_FF), F32)
    yp, sr_p, cc_p, cf_p = _layer(x_prompt, zs_ret, zs_conv, zs_ffn, 0, CHUNK, w,
                                  bt=1, t=512, r=256, fc=512)
    ys, sr_s, cc_s, cf_s = _layer(x_sample, state_ret[0], cache_conv[0], cache_ffn_conv[0], PAST_LEN, ls, w,
                                  bt=bs, t=ls, r=ls, fc=512)
    return (yp, ys, sr_p[None], cc_p[None], cf_p[None], sr_s[None], cc_s[None], cf_s[None])
```

```python
import functools

import jax
import jax.numpy as jnp
from jax import lax
from jax.experimental import pallas as pl
from jax.experimental.pallas import tpu as pltpu

D_MODEL = 1024
PAST_LEN = 4096
CHUNK = 64
RET_HEADS = 4
RET_DK = 128
RET_DV = 128
RET_WIDTH = RET_HEADS * RET_DV
CONV_WIDTH = D_MODEL - RET_WIDTH
CONV_K = 3
D_FF = 4 * D_MODEL
ROPE_BASE = 10000.0
EPS = 1e-6

F32 = jnp.float32
BF16 = jnp.bfloat16

V7X_VMEM_BYTES = 64 * 1024 * 1024
VMEM_LIMIT_BYTES = V7X_VMEM_BYTES - 8 * 1024 * 1024


def _rms(x, w):
    var = jnp.mean(x * x, axis=-1, keepdims=True)
    return x * lax.rsqrt(var + EPS) * w


def _causal_conv3(z, prev, w):
    rows = lax.broadcasted_iota(jnp.int32, z.shape, 0)
    z1 = jnp.where(rows == 0, prev[1:2], pltpu.roll(z, 1, 0))
    z2 = jnp.where(rows == 0, prev[0:1], jnp.where(rows == 1, prev[1:2], pltpu.roll(z, 2, 0)))
    return w[0:1] * z2 + w[1:2] * z1 + w[2:3] * z


def _segments(s, mc, t):
    n = min(mc, t)
    return [((s * mc + i * n) // t, (s * mc + i * n) % t, n) for i in range(mc // n)]


def _mixer_kernel(x_ref, s0_ref, cc0_ref, cosq_ref, sinq_ref, cosk_ref, sink_ref,
                  dmat_ref, fs_ref, te_ref, g_ref,
                  w_in_ref, w_out_ref, convw_ref, retw_ref, prew_ref, postw_ref,
                  x1_ref, s_ref, cc_ref, mix_ref, *, bt, t, mc, r):
    @pl.when(pl.program_id(1) == 0)
    def _():
        s_ref[...] = s0_ref[...]
        cc_ref[...] = cc0_ref[...]

    conv_tail = [cc_ref[bi] for bi in range(bt)]
    for s in range(bt * t // mc):
        segs = _segments(s, mc, t)
        mix = mix_ref.at[s % 2]
        x = jnp.concatenate([x_ref[bi, a:a + n, :] for bi, a, n in segs], axis=0)
        h = _rms(x, prew_ref[...]).astype(BF16)

        def proj(group):
            c0 = group * RET_WIDTH
            return jnp.dot(h, w_in_ref[:, c0:c0 + RET_WIDTH], preferred_element_type=F32)

        q, k, v, g, bg, cg, hc = (proj(i) for i in range(7))

        for si, (bi, a, n) in enumerate(segs):
            m0 = si * n
            cosq, sinq = cosq_ref[a:a + n, :], sinq_ref[a:a + n, :]
            cosk, sink = cosk_ref[a:a + n, :], sink_ref[a:a + n, :]
            for hd in range(RET_HEADS):
                c0 = hd * RET_DK
                qh = q[m0:m0 + n, c0:c0 + RET_DK]
                kh = k[m0:m0 + n, c0:c0 + RET_DK]
                qr = (qh * cosq + pltpu.roll(qh, RET_DK // 2, 1) * sinq).astype(BF16)
                kr = kh * cosk + pltpu.roll(kh, RET_DK // 2, 1) * sink
                kb = kr.astype(BF16)
                vb = v[m0:m0 + n, c0:c0 + RET_DV].astype(BF16)
                for ri in range(n // r):
                    sl = slice(ri * r, (ri + 1) * r)
                    state = s_ref[bi, hd]
                    scores = lax.dot_general(qr[sl], kb[sl], (((1,), (1,)), ((), ())),
                                             preferred_element_type=F32) * dmat_ref[hd]
                    intra = jnp.dot(scores.astype(BF16), vb[sl], preferred_element_type=F32)
                    cross = jnp.dot(qr[sl], state.astype(BF16), preferred_element_type=F32) * fs_ref[hd]
                    o = intra + cross
                    ke = (kr[sl] * te_ref[hd]).astype(BF16)
                    upd = lax.dot_general(ke, vb[sl], (((0,), (0,)), ((), ())),
                                          preferred_element_type=F32)
                    s_ref[bi, hd] = g_ref[hd] * state + upd
                    on = _rms(o, retw_ref[:, c0:c0 + RET_DV])
                    gh = g[m0 + ri * r:m0 + (ri + 1) * r, c0:c0 + RET_DV]
                    mix[m0 + ri * r:m0 + (ri + 1) * r, c0:c0 + RET_DV] = (on * jax.nn.silu(gh)).astype(BF16)

            z = cg[m0:m0 + n] * hc[m0:m0 + n]
            yc = _causal_conv3(z, conv_tail[bi], convw_ref[...])
            conv_tail[bi] = z[n - 2:n]
            mix[m0:m0 + n, RET_WIDTH:] = (bg[m0:m0 + n] * yc).astype(BF16)

        m = jnp.dot(mix[...], w_out_ref[...], preferred_element_type=F32)
        x1 = x + _rms(m, postw_ref[...])
        for si, (bi, a, n) in enumerate(segs):
            x1_ref[bi, a:a + n, :] = x1[si * n:(si + 1) * n]

    for bi in range(bt):
        cc_ref[bi] = conv_tail[bi]


def _ffn_kernel(x_ref, cf0_ref, w_up_ref, w_gate_ref, w_down_ref, fcw_ref, prew_ref, postw_ref,
                y_ref, cf_ref, act_ref, *, bt, t, mc, fc):
    @pl.when(pl.program_id(1) == 0)
    def _():
        cf_ref[...] = cf0_ref[...]

    nf = D_FF // fc
    conv_tail = [[cf_ref[bi, :, j * fc:(j + 1) * fc] for j in range(nf)] for bi in range(bt)]
    for s in range(bt * t // mc):
        segs = _segments(s, mc, t)
        act_buf = act_ref.at[s % 2]
        x = jnp.concatenate([x_ref[bi, a:a + n, :] for bi, a, n in segs], axis=0)
        h = _rms(x, prew_ref[...]).astype(BF16)
        for j in range(nf):
            c0 = j * fc
            u = jnp.dot(h, w_up_ref[:, c0:c0 + fc], preferred_element_type=F32)
            gate = jnp.dot(h, w_gate_ref[:, c0:c0 + fc], preferred_element_type=F32)
            w = fcw_ref[:, c0:c0 + fc]
            for si, (bi, a, n) in enumerate(segs):
                m0 = si * n
                ub = u[m0:m0 + n]
                uc = _causal_conv3(ub, conv_tail[bi][j], w)
                conv_tail[bi][j] = ub[n - 2:n]
                act = jax.nn.gelu(uc, approximate=True) * gate[m0:m0 + n]
                act_buf[m0:m0 + n, c0:c0 + fc] = act.astype(BF16)
        f = jnp.dot(act_buf[...], w_down_ref[...], preferred_element_type=F32)
        y = x + _rms(f, postw_ref[...])
        for si, (bi, a, n) in enumerate(segs):
            y_ref[bi, a:a + n, :] = y[si * n:(si + 1) * n]

    for bi in range(bt):
        for j in range(nf):
            cf_ref[bi, :, j * fc:(j + 1) * fc] = conv_tail[bi][j]


def _resident(shape):
    zeros = (0,) * len(shape)
    return pl.BlockSpec(shape, lambda b, s: zeros, pipeline_mode=pl.Buffered(1))


def _decay_tables(r, chunk):
    lg = jnp.log(1.0 - 2.0 ** (-5.0 - jnp.arange(RET_HEADS, dtype=F32)))
    idx = jnp.arange(r, dtype=F32)
    cid = jnp.arange(r) // chunk
    visible = (cid[None, :] <= cid[:, None]).astype(F32)
    dmat = jnp.exp(lg[:, None, None] * jnp.abs(idx[:, None] - idx[None, :])) * visible[None]
    from_start = jnp.exp(lg[:, None] * (idx + 1.0)[None, :])
    to_end = jnp.exp(lg[:, None] * (r - 1.0 - idx)[None, :])
    g_win = jnp.exp(lg * r)
    widen = lambda a: jnp.broadcast_to(a[:, :, None], (RET_HEADS, r, RET_DK))
    return dmat, widen(from_start), widen(to_end), g_win


def _rope_tables(pos0, length):
    inv = ROPE_BASE ** (-jnp.arange(0, RET_DK, 2, dtype=F32) / RET_DK)
    ang = (pos0 + jnp.arange(length)).astype(F32)[:, None] * inv[None, :]
    cos, sin = jnp.cos(ang), jnp.sin(ang)
    cos2 = jnp.concatenate([cos, cos], axis=-1)
    sin2 = jnp.concatenate([-sin, sin], axis=-1)
    kscale = RET_DK ** -0.5
    return cos2, sin2, cos2 * kscale, sin2 * kscale


def _mixer(x, s0, cc0, pos0, chunk, w, *, bt, t, mc, r):
    b, length, _ = x.shape
    assert b % bt == 0 and length % t == 0 and (bt * t) % mc == 0
    assert min(mc, t) % r == 0 and r % chunk == 0 and max(mc, t) % min(mc, t) == 0
    cosq, sinq, cosk, sink = _rope_tables(pos0, length)
    dmat, fs, te, g_win = _decay_tables(r, chunk)
    row = lambda n: pl.BlockSpec((t, n), lambda bi, ti: (ti, 0))
    per_stream = lambda *tail: pl.BlockSpec((bt,) + tail, lambda bi, ti: (bi,) + (0,) * len(tail))
    tile = pl.BlockSpec((bt, t, D_MODEL), lambda bi, ti: (bi, ti, 0))
    in_specs = [
        tile,
        per_stream(RET_HEADS, RET_DK, RET_DV),
        per_stream(CONV_K - 1, CONV_WIDTH),
        row(RET_DK), row(RET_DK), row(RET_DK), row(RET_DK),
        _resident(dmat.shape), _resident(fs.shape), _resident(te.shape),
        pl.BlockSpec(memory_space=pltpu.SMEM),
        _resident(w["w_in"].shape), _resident(w["w_out"].shape), _resident(w["conv_w"].shape),
        _resident(w["ret_norm_w"].shape), _resident(w["pre_mix_w"].shape), _resident(w["post_mix_w"].shape),
    ]
    out_specs = [tile, per_stream(RET_HEADS, RET_DK, RET_DV), per_stream(CONV_K - 1, CONV_WIDTH)]
    out_shape = [jax.ShapeDtypeStruct(x.shape, F32), jax.ShapeDtypeStruct(s0.shape, F32),
                 jax.ShapeDtypeStruct(cc0.shape, F32)]
    return pl.pallas_call(
        functools.partial(_mixer_kernel, bt=bt, t=t, mc=mc, r=r),
        grid=(b // bt, length // t),
        in_specs=in_specs, out_specs=out_specs, out_shape=out_shape,
        scratch_shapes=[pltpu.VMEM((2, mc, D_MODEL), BF16)],
        compiler_params=pltpu.CompilerParams(
            dimension_semantics=("arbitrary", "arbitrary"), vmem_limit_bytes=VMEM_LIMIT_BYTES),
        name="mixer",
    )(x, s0, cc0, cosq, sinq, cosk, sink, dmat, fs, te, g_win,
      w["w_in"], w["w_out"], w["conv_w"], w["ret_norm_w"], w["pre_mix_w"], w["post_mix_w"])


def _ffn(x, cf0, w, *, bt, t, mc, fc):
    b, length, _ = x.shape
    assert b % bt == 0 and length % t == 0 and (bt * t) % mc == 0 and D_FF % fc == 0
    assert max(mc, t) % min(mc, t) == 0
    per_stream = pl.BlockSpec((bt, CONV_K - 1, D_FF), lambda bi, ti: (bi, 0, 0))
    tile = pl.BlockSpec((bt, t, D_MODEL), lambda bi, ti: (bi, ti, 0))
    in_specs = [
        tile, per_stream,
        _resident(w["w_up"].shape), _resident(w["w_gate"].shape), _resident(w["w_down"].shape),
        _resident(w["ffn_conv_w"].shape), _resident(w["pre_ffn_w"].shape), _resident(w["post_ffn_w"].shape),
    ]
    return pl.pallas_call(
        functools.partial(_ffn_kernel, bt=bt, t=t, mc=mc, fc=fc),
        grid=(b // bt, length // t),
        in_specs=in_specs, out_specs=[tile, per_stream],
        out_shape=[jax.ShapeDtypeStruct(x.shape, F32), jax.ShapeDtypeStruct(cf0.shape, F32)],
        scratch_shapes=[pltpu.VMEM((2, mc, D_FF), BF16)],
        compiler_params=pltpu.CompilerParams(
            dimension_semantics=("arbitrary", "arbitrary"), vmem_limit_bytes=VMEM_LIMIT_BYTES),
        name="ffn",
    )(x, cf0, w["w_up"], w["w_gate"], w["w_down"], w["ffn_conv_w"], w["pre_ffn_w"], w["post_ffn_w"])


def _layer(x, s0, cc0, cf0, pos0, chunk, w, *, bt, t, mc, r, fc):
    x1, s_new, cc_new = _mixer(x, s0, cc0, pos0, chunk, w, bt=bt, t=t, mc=mc, r=r)
    y, cf_new = _ffn(x1, cf0, w, bt=bt, t=t, mc=mc, fc=fc)
    return y, s_new, cc_new, cf_new


def kernel(x_prompt, x_sample, state_ret, cache_conv, cache_ffn_conv, w_in, w_out, conv_w, ret_norm_w,
           pre_mix_w, post_mix_w, pre_ffn_w, post_ffn_w, w_up, w_gate, ffn_conv_w, w_down):
    assert w_in.shape[0] == 1, "single-layer kernel"
    bp, _, _ = x_prompt.shape
    bs, ls, _ = x_sample.shape
    w = {
        "w_in": w_in[0].astype(BF16), "w_out": w_out[0].astype(BF16),
        "w_up": w_up[0].astype(BF16), "w_gate": w_gate[0].astype(BF16), "w_down": w_down[0].astype(BF16),
        "conv_w": conv_w[0], "ffn_conv_w": ffn_conv_w[0],
        "ret_norm_w": ret_norm_w, "pre_mix_w": pre_mix_w, "post_mix_w": post_mix_w,
        "pre_ffn_w": pre_ffn_w, "post_ffn_w": post_ffn_w,
    }
    zs_ret = jnp.zeros((bp, RET_HEADS, RET_DK, RET_DV), F32)
    zs_conv = jnp.zeros((bp, CONV_K - 1, CONV_WIDTH), F32)
    zs_ffn = jnp.zeros((bp, CONV_K - 1, D_FF), F32)
    yp, sr_p, cc_p, cf_p = _layer(x_prompt, zs_ret, zs_conv, zs_ffn, 0, CHUNK, w,
                                  bt=1, t=1024, mc=512, r=256, fc=1024)
    ys, sr_s, cc_s, cf_s = _layer(x_sample, state_ret[0], cache_conv[0], cache_ffn_conv[0], PAST_LEN, ls, w,
                                  bt=bs, t=ls, mc=bs * ls, r=ls, fc=512)
    return (yp, ys, sr_p[None], cc_p[None], cf_p[None], sr_s[None], cc_s[None], cf_s[None])
```

```python
import functools

import jax
import jax.numpy as jnp
import numpy as np
from jax import lax
from jax.experimental import pallas as pl
from jax.experimental.pallas import tpu as pltpu

D_MODEL = 1024
PAST_LEN = 4096
CHUNK = 64
RET_HEADS = 4
RET_DK = 128
RET_DV = 128
RET_WIDTH = RET_HEADS * RET_DV
CONV_WIDTH = D_MODEL - RET_WIDTH
CONV_K = 3
D_FF = 4 * D_MODEL
ROPE_BASE = 10000.0
EPS = 1e-6

F32 = jnp.float32
BF16 = jnp.bfloat16

V7X_VMEM_BYTES = 64 * 1024 * 1024
VMEM_LIMIT_BYTES = V7X_VMEM_BYTES - 8 * 1024 * 1024

MIXER_TILE = 1024
FFN_TILE = 512
SUB_TILE = 512
RET_WINDOW = 256
FFN_COLS = 1024


def _rms(x, w):
    var = jnp.mean(x * x, axis=-1, keepdims=True)
    return x * lax.rsqrt(var + EPS) * w


def _causal_conv3(z, prev, w):
    rows = lax.broadcasted_iota(jnp.int32, z.shape, 0)
    z1 = jnp.where(rows == 0, prev[1:2], pltpu.roll(z, 1, 0))
    z2 = jnp.where(rows == 0, prev[0:1], jnp.where(rows == 1, prev[1:2], pltpu.roll(z, 2, 0)))
    return w[0:1] * z2 + w[1:2] * z1 + w[2:3] * z


def _segments(s, mc, t):
    n = min(mc, t)
    return [((s * mc + i * n) // t, (s * mc + i * n) % t, n) for i in range(mc // n)]


def _mixer_tile(x_ref, x1_ref, s_ref, cc_ref, rope, decay, g_ref, w_in_ref, w_out_ref, small, mix_ref,
                *, bt, t, mc, r):
    cosq_ref, sinq_ref, cosk_ref, sink_ref = rope
    dmat_ref, fs_ref, te_ref = decay
    convw_ref, retw_ref, prew_ref, postw_ref = small
    conv_tail = [cc_ref[bi] for bi in range(bt)]
    for s in range(bt * t // mc):
        segs = _segments(s, mc, t)
        mix = mix_ref.at[s % 2, 0:mc]
        x = jnp.concatenate([x_ref[bi, a:a + n, :] for bi, a, n in segs], axis=0)
        h = _rms(x, prew_ref[...]).astype(BF16)

        def proj(group):
            c0 = group * RET_WIDTH
            return jnp.dot(h, w_in_ref[:, c0:c0 + RET_WIDTH], preferred_element_type=F32)

        q, k, v, g, bg, cg, hc = (proj(i) for i in range(7))

        for si, (bi, a, n) in enumerate(segs):
            m0 = si * n
            cosq, sinq = cosq_ref[a:a + n, :], sinq_ref[a:a + n, :]
            cosk, sink = cosk_ref[a:a + n, :], sink_ref[a:a + n, :]
            for hd in range(RET_HEADS):
                c0 = hd * RET_DK
                qh = q[m0:m0 + n, c0:c0 + RET_DK]
                kh = k[m0:m0 + n, c0:c0 + RET_DK]
                qr = (qh * cosq + pltpu.roll(qh, RET_DK // 2, 1) * sinq).astype(BF16)
                kr = kh * cosk + pltpu.roll(kh, RET_DK // 2, 1) * sink
                kb = kr.astype(BF16)
                vb = v[m0:m0 + n, c0:c0 + RET_DV].astype(BF16)
                for ri in range(n // r):
                    sl = slice(ri * r, (ri + 1) * r)
                    state = s_ref[bi, hd]
                    scores = lax.dot_general(qr[sl], kb[sl], (((1,), (1,)), ((), ())),
                                             preferred_element_type=F32) * dmat_ref[hd]
                    intra = jnp.dot(scores.astype(BF16), vb[sl], preferred_element_type=F32)
                    cross = jnp.dot(qr[sl], state.astype(BF16), preferred_element_type=F32) * fs_ref[hd]
                    o = intra + cross
                    ke = (kr[sl] * te_ref[hd]).astype(BF16)
                    upd = lax.dot_general(ke, vb[sl], (((0,), (0,)), ((), ())),
                                          preferred_element_type=F32)
                    s_ref[bi, hd] = g_ref[hd] * state + upd
                    on = _rms(o, retw_ref[:, c0:c0 + RET_DV])
                    gh = g[m0 + ri * r:m0 + (ri + 1) * r, c0:c0 + RET_DV]
                    mix[m0 + ri * r:m0 + (ri + 1) * r, c0:c0 + RET_DV] = (on * jax.nn.silu(gh)).astype(BF16)

            z = cg[m0:m0 + n] * hc[m0:m0 + n]
            yc = _causal_conv3(z, conv_tail[bi], convw_ref[...])
            conv_tail[bi] = z[n - 2:n]
            mix[m0:m0 + n, RET_WIDTH:] = (bg[m0:m0 + n] * yc).astype(BF16)

        m = jnp.dot(mix[...], w_out_ref[...], preferred_element_type=F32)
        x1 = x + _rms(m, postw_ref[...])
        for si, (bi, a, n) in enumerate(segs):
            x1_ref[bi, a:a + n, :] = x1[si * n:(si + 1) * n]

    for bi in range(bt):
        cc_ref[bi] = conv_tail[bi]


def _ffn_tile(x_ref, y_ref, cf_ref, w_up_ref, w_gate_ref, w_down_ref, fcw_ref, prew_ref, postw_ref, act_ref,
              *, bt, t, mc, fc):
    nf = D_FF // fc
    conv_tail = [[cf_ref[bi, :, j * fc:(j + 1) * fc] for j in range(nf)] for bi in range(bt)]
    for s in range(bt * t // mc):
        segs = _segments(s, mc, t)
        act_buf = act_ref.at[s % 2, 0:mc]
        x = jnp.concatenate([x_ref[bi, a:a + n, :] for bi, a, n in segs], axis=0)
        h = _rms(x, prew_ref[...]).astype(BF16)
        for j in range(nf):
            c0 = j * fc
            u = jnp.dot(h, w_up_ref[:, c0:c0 + fc], preferred_element_type=F32)
            gate = jnp.dot(h, w_gate_ref[:, c0:c0 + fc], preferred_element_type=F32)
            w = fcw_ref[:, c0:c0 + fc]
            for si, (bi, a, n) in enumerate(segs):
                m0 = si * n
                ub = u[m0:m0 + n]
                uc = _causal_conv3(ub, conv_tail[bi][j], w)
                conv_tail[bi][j] = ub[n - 2:n]
                act = jax.nn.gelu(uc, approximate=True) * gate[m0:m0 + n]
                act_buf[m0:m0 + n, c0:c0 + fc] = act.astype(BF16)
        f = jnp.dot(act_buf[...], w_down_ref[...], preferred_element_type=F32)
        y = x + _rms(f, postw_ref[...])
        for si, (bi, a, n) in enumerate(segs):
            y_ref[bi, a:a + n, :] = y[si * n:(si + 1) * n]

    for bi in range(bt):
        for j in range(nf):
            cf_ref[bi, :, j * fc:(j + 1) * fc] = conv_tail[bi][j]


def _mixer_kernel(xp_ref, xs_ref, s0_ref, cc0_ref,
                  p_cosq, p_sinq, p_cosk, p_sink, p_dmat, p_fs, p_te, p_g,
                  s_cosq, s_sinq, s_cosk, s_sink, s_dmat, s_fs, s_te, s_g,
                  w_in_ref, w_out_ref, convw_ref, retw_ref, prew_ref, postw_ref,
                  up_ref, gate_ref, down_ref,
                  x1p_ref, x1s_ref, sp_ref, ccp_ref, ss_ref, ccs_ref, upb_ref, gateb_ref, downb_ref,
                  mix_ref, *, n_prompt, tiles_per_stream, t, mc, r, bs, ls):
    i = pl.program_id(0)
    small = (convw_ref, retw_ref, prew_ref, postw_ref)

    @pl.when(i < n_prompt)
    def _():
        @pl.when(i % tiles_per_stream == 0)
        def _():
            sp_ref[...] = jnp.zeros_like(sp_ref)
            ccp_ref[...] = jnp.zeros_like(ccp_ref)

        upb_ref[...] = up_ref[...].astype(BF16)
        gateb_ref[...] = gate_ref[...].astype(BF16)
        downb_ref[...] = down_ref[...].astype(BF16)
        _mixer_tile(xp_ref, x1p_ref, sp_ref, ccp_ref, (p_cosq, p_sinq, p_cosk, p_sink), (p_dmat, p_fs, p_te), p_g,
                    w_in_ref, w_out_ref, small, mix_ref, bt=1, t=t, mc=mc, r=r)

    @pl.when(i == n_prompt)
    def _():
        ss_ref[...] = s0_ref[...]
        ccs_ref[...] = cc0_ref[...]
        _mixer_tile(xs_ref, x1s_ref, ss_ref, ccs_ref, (s_cosq, s_sinq, s_cosk, s_sink), (s_dmat, s_fs, s_te), s_g,
                    w_in_ref, w_out_ref, small, mix_ref, bt=bs, t=ls, mc=bs * ls, r=ls)


def _ffn_prompt_kernel(x_ref, w_up_ref, w_gate_ref, w_down_ref, fcw_ref, prew_ref, postw_ref,
                       y_ref, cf_ref, act_ref, *, tiles_per_stream, t, mc, fc):
    @pl.when(pl.program_id(0) % tiles_per_stream == 0)
    def _():
        cf_ref[...] = jnp.zeros_like(cf_ref)

    _ffn_tile(x_ref, y_ref, cf_ref, w_up_ref, w_gate_ref, w_down_ref, fcw_ref, prew_ref, postw_ref, act_ref,
              bt=1, t=t, mc=mc, fc=fc)


def _ffn_sample_kernel(x_ref, cf0_ref, w_up_ref, w_gate_ref, w_down_ref, fcw_ref, prew_ref, postw_ref,
                       y_ref, cf_ref, act_ref, *, bs, ls, fc):
    cf_ref[...] = cf0_ref[...]
    _ffn_tile(x_ref, y_ref, cf_ref, w_up_ref, w_gate_ref, w_down_ref, fcw_ref, prew_ref, postw_ref, act_ref,
              bt=bs, t=ls, mc=bs * ls, fc=fc)


def _decay_tables(r, chunk):
    lg = np.log(1.0 - 2.0 ** (-5.0 - np.arange(RET_HEADS, dtype=np.float64)))
    idx = np.arange(r, dtype=np.float64)
    cid = np.arange(r) // chunk
    visible = cid[None, :] <= cid[:, None]
    dmat = np.exp(lg[:, None, None] * np.abs(idx[:, None] - idx[None, :])) * visible[None]
    from_start = np.exp(lg[:, None] * (idx + 1.0)[None, :])
    to_end = np.exp(lg[:, None] * (r - 1.0 - idx)[None, :])
    g_win = np.exp(lg * r)
    widen = lambda a: np.broadcast_to(a[:, :, None], (RET_HEADS, r, RET_DK))
    return tuple(jnp.asarray(a, F32) for a in (dmat, widen(from_start), widen(to_end), g_win))


def _rope_tables(pos0, length):
    inv = ROPE_BASE ** (-np.arange(0, RET_DK, 2, dtype=np.float64) / RET_DK)
    ang = (pos0 + np.arange(length, dtype=np.float64))[:, None] * inv[None, :]
    cos2 = np.concatenate([np.cos(ang), np.cos(ang)], axis=-1)
    sin2 = np.concatenate([-np.sin(ang), np.sin(ang)], axis=-1)
    kscale = RET_DK ** -0.5
    return tuple(jnp.asarray(a, F32) for a in (cos2, sin2, cos2 * kscale, sin2 * kscale))


def _whole(shape, buffers=1):
    zeros = (0,) * len(shape)
    mode = pl.Buffered(1) if buffers == 1 else None
    return pl.BlockSpec(shape, lambda i: zeros, pipeline_mode=mode)


def _prompt_maps(n_prompt, tiles_per_stream):
    step = lambda i: jnp.minimum(i, n_prompt - 1)
    tile = lambda i: (step(i) // tiles_per_stream, step(i) % tiles_per_stream, 0)
    rows = lambda i: (step(i) % tiles_per_stream, 0)
    stream = lambda ndim: (lambda i: (step(i) // tiles_per_stream,) + (0,) * (ndim - 1))
    slab = lambda i: (step(i), 0)
    return tile, rows, stream, slab


def _mixer(x_prompt, x_sample, state_ret, cache_conv, w, ffn_w32):
    bp, lp, _ = x_prompt.shape
    bs, ls, _ = x_sample.shape
    t, mc, r = MIXER_TILE, SUB_TILE, RET_WINDOW
    assert lp % t == 0 and t % mc == 0 and mc % r == 0 and r % CHUNK == 0 and (bs * ls) % 8 == 0
    tiles_per_stream = lp // t
    n_prompt = bp * tiles_per_stream
    assert D_MODEL % n_prompt == 0 and D_FF % n_prompt == 0
    tile, rows, stream, slab = _prompt_maps(n_prompt, tiles_per_stream)
    p_rope = _rope_tables(0, lp)
    p_decay = _decay_tables(r, CHUNK)
    s_rope = _rope_tables(PAST_LEN, ls)
    s_decay = _decay_tables(ls, ls)
    smem = pl.BlockSpec(memory_space=pltpu.SMEM)
    up_rows, down_rows = D_MODEL // n_prompt, D_FF // n_prompt

    in_specs = [
        pl.BlockSpec((1, t, D_MODEL), tile), _whole(x_sample.shape), _whole(state_ret.shape), _whole(cache_conv.shape),
        *[pl.BlockSpec((t, RET_DK), rows) for _ in p_rope],
        *[_whole(a.shape) for a in p_decay[:3]], smem,
        *[_whole(a.shape) for a in s_rope],
        *[_whole(a.shape) for a in s_decay[:3]], smem,
        _whole(w["w_in"].shape), _whole(w["w_out"].shape), _whole(w["conv_w"].shape),
        _whole(w["ret_norm_w"].shape), _whole(w["pre_mix_w"].shape), _whole(w["post_mix_w"].shape),
        pl.BlockSpec((up_rows, D_FF), slab), pl.BlockSpec((up_rows, D_FF), slab),
        pl.BlockSpec((down_rows, D_MODEL), slab),
    ]
    out_specs = [
        pl.BlockSpec((1, t, D_MODEL), tile), _whole(x_sample.shape, 2),
        pl.BlockSpec((1, RET_HEADS, RET_DK, RET_DV), stream(4)), pl.BlockSpec((1, CONV_K - 1, CONV_WIDTH), stream(3)),
        _whole(state_ret.shape, 2), _whole(cache_conv.shape, 2),
        pl.BlockSpec((up_rows, D_FF), slab), pl.BlockSpec((up_rows, D_FF), slab),
        pl.BlockSpec((down_rows, D_MODEL), slab),
    ]
    out_shape = [
        jax.ShapeDtypeStruct(x_prompt.shape, F32), jax.ShapeDtypeStruct(x_sample.shape, F32),
        jax.ShapeDtypeStruct((bp, RET_HEADS, RET_DK, RET_DV), F32),
        jax.ShapeDtypeStruct((bp, CONV_K - 1, CONV_WIDTH), F32),
        jax.ShapeDtypeStruct(state_ret.shape, F32), jax.ShapeDtypeStruct(cache_conv.shape, F32),
        jax.ShapeDtypeStruct((D_MODEL, D_FF), BF16), jax.ShapeDtypeStruct((D_MODEL, D_FF), BF16),
        jax.ShapeDtypeStruct((D_FF, D_MODEL), BF16),
    ]
    return pl.pallas_call(
        functools.partial(_mixer_kernel, n_prompt=n_prompt, tiles_per_stream=tiles_per_stream,
                          t=t, mc=mc, r=r, bs=bs, ls=ls),
        grid=(n_prompt + 1,),
        in_specs=in_specs, out_specs=out_specs, out_shape=out_shape,
        scratch_shapes=[pltpu.VMEM((2, max(mc, bs * ls), D_MODEL), BF16)],
        compiler_params=pltpu.CompilerParams(
            dimension_semantics=("arbitrary",), vmem_limit_bytes=VMEM_LIMIT_BYTES),
        name="mixer",
    )(x_prompt, x_sample, state_ret, cache_conv,
      *p_rope, *p_decay, *s_rope, *s_decay,
      w["w_in"], w["w_out"], w["conv_w"], w["ret_norm_w"], w["pre_mix_w"], w["post_mix_w"],
      *ffn_w32)


def _ffn(x_prompt, x_sample, cache_ffn, w_up, w_gate, w_down, w):
    bp, lp, _ = x_prompt.shape
    bs, ls, _ = x_sample.shape
    t, mc, fc = FFN_TILE, SUB_TILE, FFN_COLS
    assert lp % t == 0 and t % mc == 0 and D_FF % fc == 0 and (bs * ls) % 8 == 0
    tiles_per_stream = lp // t
    n_prompt = bp * tiles_per_stream
    tile, _, stream, _ = _prompt_maps(n_prompt, tiles_per_stream)
    weights = (w_up, w_gate, w_down, w["ffn_conv_w"], w["pre_ffn_w"], w["post_ffn_w"])
    weight_specs = [_whole(a.shape) for a in weights]
    params = pltpu.CompilerParams(dimension_semantics=("arbitrary",), vmem_limit_bytes=VMEM_LIMIT_BYTES)
    yp, cf_p = pl.pallas_call(
        functools.partial(_ffn_prompt_kernel, tiles_per_stream=tiles_per_stream, t=t, mc=mc, fc=fc),
        grid=(n_prompt,),
        in_specs=[pl.BlockSpec((1, t, D_MODEL), tile), *weight_specs],
        out_specs=[pl.BlockSpec((1, t, D_MODEL), tile), pl.BlockSpec((1, CONV_K - 1, D_FF), stream(3))],
        out_shape=[jax.ShapeDtypeStruct(x_prompt.shape, F32), jax.ShapeDtypeStruct((bp, CONV_K - 1, D_FF), F32)],
        scratch_shapes=[pltpu.VMEM((min(2, t // mc), mc, D_FF), BF16)],
        compiler_params=params,
        name="ffn_prompt",
    )(x_prompt, *weights)
    ys, cf_s = pl.pallas_call(
        functools.partial(_ffn_sample_kernel, bs=bs, ls=ls, fc=fc),
        grid=(1,),
        in_specs=[_whole(x_sample.shape), _whole(cache_ffn.shape), *weight_specs],
        out_specs=[_whole(x_sample.shape, 2), _whole(cache_ffn.shape, 2)],
        out_shape=[jax.ShapeDtypeStruct(x_sample.shape, F32), jax.ShapeDtypeStruct(cache_ffn.shape, F32)],
        scratch_shapes=[pltpu.VMEM((1, bs * ls, D_FF), BF16)],
        compiler_params=params,
        name="ffn_sample",
    )(x_sample, cache_ffn, *weights)
    return yp, ys, cf_p, cf_s


def kernel(x_prompt, x_sample, state_ret, cache_conv, cache_ffn_conv, w_in, w_out, conv_w, ret_norm_w,
           pre_mix_w, post_mix_w, pre_ffn_w, post_ffn_w, w_up, w_gate, ffn_conv_w, w_down):
    assert w_in.shape[0] == 1, "single-layer kernel"
    w = {
        "w_in": w_in[0].astype(BF16), "w_out": w_out[0].astype(BF16),
        "conv_w": conv_w[0], "ffn_conv_w": ffn_conv_w[0],
        "ret_norm_w": ret_norm_w, "pre_mix_w": pre_mix_w, "post_mix_w": post_mix_w,
        "pre_ffn_w": pre_ffn_w, "post_ffn_w": post_ffn_w,
    }
    x1p, x1s, sr_p, cc_p, sr_s, cc_s, up_b, gate_b, down_b = _mixer(
        x_prompt, x_sample, state_ret[0], cache_conv[0], w, (w_up[0], w_gate[0], w_down[0]))
    yp, ys, cf_p, cf_s = _ffn(x1p, x1s, cache_ffn_conv[0], up_b, gate_b, down_b, w)
    return (yp, ys, sr_p[None], cc_p[None], cf_p[None], sr_s[None], cc_s[None], cf_s[None])
```
